```python
import jax, jax.numpy as jnp
from jax import lax
import numpy as np

D_MODEL = 2048
BATCH = 1
SEQ = 16384
DEPTH = 1
DEC_BATCH = 16
DEC_SEQ = 64
PAST_LEN = 4096

CHUNK = 64
D_A = D_MODEL // 2
D_B = D_MODEL - D_A
CONV_A_W = 3
CONV_B_W = 31
D_IN = 3 * D_A + 2 * D_B
D_FF = 5632
EPS = 1e-6

kernel_name = "hybrid_shortconv_conformer_stream_step"


def _rmsnorm(x, g):
    x32 = x.astype(jnp.float32)
    y = x32 * lax.rsqrt(jnp.mean(x32 * x32, axis=-1, keepdims=True) + EPS)
    return (y * g.astype(jnp.float32)).astype(x.dtype)


def _layernorm(x, g, b):
    x32 = x.astype(jnp.float32)
    mu = jnp.mean(x32, axis=-1, keepdims=True)
    xc = x32 - mu
    var = jnp.mean(xc * xc, axis=-1, keepdims=True)
    y = xc * lax.rsqrt(var + EPS)
    return (y * g.astype(jnp.float32) + b.astype(jnp.float32)).astype(x.dtype)


def _swiglu(h, w_gate, w_up, w_down):
    return (jax.nn.silu(h @ w_gate) * (h @ w_up)) @ w_down


def _causal_dwconv(u, past, w):
    width = w.shape[0]
    ext = jnp.concatenate([past.astype(u.dtype), u], axis=1)
    out = lax.conv_general_dilated(
        ext, w[:, None, :].astype(u.dtype), window_strides=(1,), padding="VALID",
        dimension_numbers=("NWC", "WIO", "NWC"), feature_group_count=u.shape[-1])
    return out, ext[:, ext.shape[1] - (width - 1):, :]


def _layer(x, st_a, st_b, ffn1_norm, ffn1_w_gate, ffn1_w_up, ffn1_w_down, mix_norm, w_in,
           conv_a_w, conv_b_w, conv_b_bias, conv_b_ln_g, conv_b_ln_b, w_out,
           ffn2_norm, ffn2_w_gate, ffn2_w_up, ffn2_w_down):
    x = x + 0.5 * _swiglu(_rmsnorm(x, ffn1_norm), ffn1_w_gate, ffn1_w_up, ffn1_w_down)
    h = _rmsnorm(x, mix_norm)
    z = h @ w_in
    b_gate, c_gate, v_a, u_b, g_b = jnp.split(
        z, [D_A, 2 * D_A, 3 * D_A, 3 * D_A + D_B], axis=-1)
    conv_a, new_a = _causal_dwconv(c_gate * v_a, st_a, conv_a_w)
    y_a = b_gate * conv_a
    u = u_b * jax.nn.sigmoid(g_b)
    conv_b, new_b = _causal_dwconv(u, st_b, conv_b_w)
    y_b = jax.nn.silu(_layernorm(conv_b + conv_b_bias.astype(conv_b.dtype), conv_b_ln_g, conv_b_ln_b))
    x = x + jnp.concatenate([y_a, y_b], axis=-1) @ w_out
    x = x + 0.5 * _swiglu(_rmsnorm(x, ffn2_norm), ffn2_w_gate, ffn2_w_up, ffn2_w_down)
    return x, new_a, new_b


def setup_inputs(seed: int = 0) -> dict:
    key = jax.random.key(seed)
    ks = jax.random.split(key, 20)
    f32 = jnp.float32
    nrm = lambda k, shape, scale: jax.random.normal(k, shape, f32) * scale
    gain = lambda k, shape: 1.0 + 0.01 * jax.random.normal(k, shape, f32)
    return {
        "x_prompt": nrm(ks[0], (BATCH, SEQ, D_MODEL), 1.0),
        "x_sample": nrm(ks[1], (DEC_BATCH, DEC_SEQ, D_MODEL), 1.0),
        "state_conv_a": nrm(ks[2], (DEPTH, DEC_BATCH, CONV_A_W - 1, D_A), 1.0),
        "state_conv_b": nrm(ks[3], (DEPTH, DEC_BATCH, CONV_B_W - 1, D_B), 1.0),
        "ffn1_norm": gain(ks[4], (DEPTH, D_MODEL)),
        "ffn1_w_gate": nrm(ks[5], (DEPTH, D_MODEL, D_FF), D_MODEL ** -0.5),
        "ffn1_w_up": nrm(ks[6], (DEPTH, D_MODEL, D_FF), D_MODEL ** -0.5),
        "ffn1_w_down": nrm(ks[7], (DEPTH, D_FF, D_MODEL), D_FF ** -0.5),
        "mix_norm": gain(ks[8], (DEPTH, D_MODEL)),
        "w_in": nrm(ks[9], (DEPTH, D_MODEL, D_IN), D_MODEL ** -0.5),
        "conv_a_w": nrm(ks[10], (DEPTH, CONV_A_W, D_A), CONV_A_W ** -0.5),
        "conv_b_w": nrm(ks[11], (DEPTH, CONV_B_W, D_B), CONV_B_W ** -0.5),
        "conv_b_bias": nrm(ks[12], (DEPTH, D_B), 0.01),
        "conv_b_ln_g": gain(ks[13], (DEPTH, D_B)),
        "conv_b_ln_b": nrm(ks[14], (DEPTH, D_B), 0.01),
        "w_out": nrm(ks[15], (DEPTH, D_A + D_B, D_MODEL), (D_A + D_B) ** -0.5),
        "ffn2_norm": gain(ks[16], (DEPTH, D_MODEL)),
        "ffn2_w_gate": nrm(ks[17], (DEPTH, D_MODEL, D_FF), D_MODEL ** -0.5),
        "ffn2_w_up": nrm(ks[18], (DEPTH, D_MODEL, D_FF), D_MODEL ** -0.5),
        "ffn2_w_down": nrm(ks[19], (DEPTH, D_FF, D_MODEL), D_FF ** -0.5),
        "final_norm": gain(jax.random.fold_in(key, 99), (D_MODEL,)),
    }


def reference(x_prompt, x_sample, state_conv_a, state_conv_b, ffn1_norm, ffn1_w_gate,
              ffn1_w_up, ffn1_w_down, mix_norm, w_in, conv_a_w, conv_b_w, conv_b_bias,
              conv_b_ln_g, conv_b_ln_b, w_out, ffn2_norm, ffn2_w_gate, ffn2_w_up,
              ffn2_w_down, final_norm):
    bp = x_prompt.shape[0]
    xp, xs = x_prompt, x_sample
    pa_list, pb_list, sa_list, sb_list = [], [], [], []
    for l in range(DEPTH):
        w = (ffn1_norm[l], ffn1_w_gate[l], ffn1_w_up[l], ffn1_w_down[l], mix_norm[l], w_in[l],
             conv_a_w[l], conv_b_w[l], conv_b_bias[l], conv_b_ln_g[l], conv_b_ln_b[l], w_out[l],
             ffn2_norm[l], ffn2_w_gate[l], ffn2_w_up[l], ffn2_w_down[l])
        zero_a = jnp.zeros((bp, CONV_A_W - 1, D_A), xp.dtype)
        zero_b = jnp.zeros((bp, CONV_B_W - 1, D_B), xp.dtype)
        xp, pa, pb = _layer(xp, zero_a, zero_b, *w)
        xs, sa, sb = _layer(xs, state_conv_a[l], state_conv_b[l], *w)
        pa_list.append(pa)
        pb_list.append(pb)
        sa_list.append(sa)
        sb_list.append(sb)
    y_prompt = _rmsnorm(xp, final_norm)
    y_sample = _rmsnorm(xs, final_norm)
    new_conv_a_prompt = jnp.stack(pa_list, axis=0)
    new_conv_b_prompt = jnp.stack(pb_list, axis=0)
    new_conv_a_sample = jnp.stack(sa_list, axis=0)
    new_conv_b_sample = jnp.stack(sb_list, axis=0)
    return (y_prompt, y_sample, new_conv_a_prompt, new_conv_b_prompt, new_conv_a_sample, new_conv_b_sample)
```

```python
import functools

import jax
import jax.numpy as jnp
from jax import lax
from jax.experimental import pallas as pl
from jax.experimental.pallas import tpu as pltpu

EPS = 1e-6
CONV_A_W = 3
CONV_B_W = 31

SUBLANES = 8
LANES = 128
HALO_A = SUBLANES
HALO_B = 4 * SUBLANES
CONV_ROWS = 64
VMEM_LIMIT_BYTES = 60 * 1024 * 1024

F32 = jnp.float32
BF16 = jnp.bfloat16


def _rmsnorm(x, g):
    ms = jnp.mean(x * x, axis=-1, keepdims=True)
    return x * lax.rsqrt(ms + EPS) * g


def _ffn_kernel(x_ref, g_ref, wg_ref, wu_ref, wd_ref, fg_ref, o_ref, xn_ref, *, final_norm):
    f = pl.program_id(1)

    @pl.when(f == 0)
    def _():
        xn_ref[...] = _rmsnorm(x_ref[...], g_ref[...]).astype(BF16)
        o_ref[...] = jnp.zeros_like(o_ref)

    xn = xn_ref[...]
    gate = jnp.dot(xn, wg_ref[...], preferred_element_type=F32)
    up = jnp.dot(xn, wu_ref[...], preferred_element_type=F32)
    h = (gate * jax.nn.sigmoid(gate) * up).astype(BF16)
    o_ref[...] += jnp.dot(h, wd_ref[...], preferred_element_type=F32)

    @pl.when(f == pl.num_programs(1) - 1)
    def _():
        y = x_ref[...] + 0.5 * o_ref[...]
        if final_norm:
            y = _rmsnorm(y, fg_ref[...])
        o_ref[...] = y


def _ffn(x2d, norm_g, wg, wu, wd, final_g, *, final_norm, tm, tf):
    m, d = x2d.shape
    dff = wg.shape[1]
    assert m % tm == 0 and dff % tf == 0
    return pl.pallas_call(
        functools.partial(_ffn_kernel, final_norm=final_norm),
        out_shape=jax.ShapeDtypeStruct((m, d), F32),
        grid=(m // tm, dff // tf),
        in_specs=[
            pl.BlockSpec((tm, d), lambda i, f: (i, 0)),
            pl.BlockSpec((1, d), lambda i, f: (0, 0)),
            pl.BlockSpec((d, tf), lambda i, f: (0, f)),
            pl.BlockSpec((d, tf), lambda i, f: (0, f)),
            pl.BlockSpec((tf, d), lambda i, f: (f, 0)),
            pl.BlockSpec((1, d), lambda i, f: (0, 0)),
        ],
        out_specs=pl.BlockSpec((tm, d), lambda i, f: (i, 0)),
        scratch_shapes=[pltpu.VMEM((tm, d), BF16)],
        compiler_params=pltpu.CompilerParams(
            dimension_semantics=("arbitrary", "arbitrary"),
            vmem_limit_bytes=VMEM_LIMIT_BYTES,
        ),
        name="ffn_final" if final_norm else "ffn",
    )(x2d, norm_g, wg, wu, wd, final_g)


def _shifted_sum(parts, rows):
    out = parts[0][SUBLANES:SUBLANES + rows]
    for b in range(1, len(parts)):
        out = out + parts[b][SUBLANES - b:SUBLANES - b + rows]
    return out


def _mixer_kernel(x_ref, sa_ref, sb_ref, g_ref, win_ref, caw_ref, cbw_ref, cbb_ref, lng_ref,
                  lnb_ref, wout_ref, o_ref, na_ref, nb_ref,
                  hn_ref, exta_ref, extb_ref, conv_ref, y_ref, *, nseg, seg, cw):
    t = pl.program_id(1)
    tm = nseg * seg
    d_a = exta_ref.shape[-1]
    d_b = extb_ref.shape[-1]
    stride_a = HALO_A + seg
    stride_b = HALO_B + seg
    chunks_per_seg = seg // CONV_ROWS
    n_items = tm // CONV_ROWS

    @pl.when(t == 0)
    def _():
        for s in range(nseg):
            exta_ref[s * stride_a:s * stride_a + HALO_A, :] = sa_ref[s]
            extb_ref[s * stride_b:s * stride_b + HALO_B, :] = sb_ref[s]

    @pl.when(t > 0)
    def _():
        for s in range(nseg):
            exta_ref[s * stride_a:s * stride_a + HALO_A, :] = (
                exta_ref[s * stride_a + seg:s * stride_a + seg + HALO_A, :])
            extb_ref[s * stride_b:s * stride_b + HALO_B, :] = (
                extb_ref[s * stride_b + seg:s * stride_b + seg + HALO_B, :])

    d_model = x_ref.shape[-1]
    hn_ref[...] = _rmsnorm(x_ref[...].reshape(tm, d_model), g_ref[...]).astype(BF16)

    def proj(col0):
        return jnp.dot(hn_ref[...], win_ref[:, col0:col0 + cw], preferred_element_type=F32)

    def item_rows(i, stride, halo):
        if nseg == 1:
            s, r = 0, i * CONV_ROWS
        else:
            s = i // chunks_per_seg
            r = (i - s * chunks_per_seg) * CONV_ROWS
        return pl.multiple_of(s * stride + r, SUBLANES), pl.multiple_of(s * seg + r, SUBLANES)

    for c in range(d_a // cw):
        cv = proj(d_a + c * cw) * proj(2 * d_a + c * cw)
        for s in range(nseg):
            exta_ref[s * stride_a + HALO_A:(s + 1) * stride_a, c * cw:(c + 1) * cw] = (
                cv[s * seg:(s + 1) * seg])

    def conv_a_item(i, carry):
        e0, r0 = item_rows(i, stride_a, HALO_A)
        for l in range(d_a // LANES):
            ls = slice(l * LANES, (l + 1) * LANES)
            win = exta_ref[pl.ds(e0, CONV_ROWS + HALO_A), ls]
            out = win[HALO_A:HALO_A + CONV_ROWS] * caw_ref[CONV_A_W - 1:CONV_A_W, ls]
            for j in range(1, CONV_A_W):
                out = out + (win[HALO_A - j:HALO_A - j + CONV_ROWS]
                             * caw_ref[CONV_A_W - 1 - j:CONV_A_W - j, ls])
            conv_ref[pl.ds(r0, CONV_ROWS), ls] = out
        return carry

    lax.fori_loop(0, n_items, conv_a_item, 0)

    for c in range(d_a // cw):
        y_ref[:, c * cw:(c + 1) * cw] = (
            proj(c * cw) * conv_ref[:, c * cw:(c + 1) * cw]).astype(BF16)

    for c in range(d_b // cw):
        u = proj(3 * d_a + c * cw)
        u = u * jax.nn.sigmoid(proj(3 * d_a + d_b + c * cw))
        for s in range(nseg):
            extb_ref[s * stride_b + HALO_B:(s + 1) * stride_b, c * cw:(c + 1) * cw] = (
                u[s * seg:(s + 1) * seg])

    n_groups = HALO_B // SUBLANES

    def conv_b_item(i, carry):
        e0, r0 = item_rows(i, stride_b, HALO_B)
        for l in range(d_b // LANES):
            ls = slice(l * LANES, (l + 1) * LANES)
            win = extb_ref[pl.ds(e0, CONV_ROWS + HALO_B), ls]
            parts = []
            for b in range(SUBLANES):
                acc = None
                for a in range(n_groups):
                    j = SUBLANES * a + b
                    if j >= CONV_B_W:
                        continue
                    lo = HALO_B - SUBLANES - SUBLANES * a
                    term = (win[lo:lo + CONV_ROWS + SUBLANES]
                            * cbw_ref[CONV_B_W - 1 - j:CONV_B_W - j, ls])
                    acc = term if acc is None else acc + term
                parts.append(acc)
            conv_ref[pl.ds(r0, CONV_ROWS), ls] = _shifted_sum(parts, CONV_ROWS) + cbb_ref[:, ls]
        v = conv_ref[pl.ds(r0, CONV_ROWS), :]
        mu = jnp.mean(v, axis=-1, keepdims=True)
        vc = v - mu
        var = jnp.mean(vc * vc, axis=-1, keepdims=True)
        n = vc * lax.rsqrt(var + EPS) * lng_ref[...] + lnb_ref[...]
        y_ref[pl.ds(r0, CONV_ROWS), d_a:d_a + d_b] = (n * jax.nn.sigmoid(n)).astype(BF16)
        return carry

    lax.fori_loop(0, n_items, conv_b_item, 0)

    out = jnp.dot(y_ref[...], wout_ref[...], preferred_element_type=F32)
    o_ref[...] = x_ref[...] + out.reshape(o_ref.shape)

    for s in range(nseg):
        na_ref[s] = exta_ref[(s + 1) * stride_a - HALO_A:(s + 1) * stride_a, :]
        nb_ref[s] = extb_ref[(s + 1) * stride_b - HALO_B:(s + 1) * stride_b, :]


def _mixer(x3d, state_a, state_b, norm_g, w_in, conv_a_w, conv_b_w, conv_b_bias, ln_g, ln_b,
           w_out, *, nseg, seg, cw=256):
    b, s_len, d = x3d.shape
    d_a = conv_a_w.shape[-1]
    d_b = conv_b_w.shape[-1]
    assert b % nseg == 0 and s_len % seg == 0 and seg % CONV_ROWS == 0
    assert seg >= HALO_B and (nseg == 1 or s_len == seg)
    tm = nseg * seg
    const = lambda g, t: (0, 0)
    single = dict(pipeline_mode=pl.Buffered(1))
    return pl.pallas_call(
        functools.partial(_mixer_kernel, nseg=nseg, seg=seg, cw=cw),
        out_shape=(
            jax.ShapeDtypeStruct((b, s_len, d), F32),
            jax.ShapeDtypeStruct((b, HALO_A, d_a), F32),
            jax.ShapeDtypeStruct((b, HALO_B, d_b), F32),
        ),
        grid=(b // nseg, s_len // seg),
        in_specs=[
            pl.BlockSpec((nseg, seg, d), lambda g, t: (g, t, 0)),
            pl.BlockSpec((nseg, HALO_A, d_a), lambda g, t: (g, 0, 0)),
            pl.BlockSpec((nseg, HALO_B, d_b), lambda g, t: (g, 0, 0)),
            pl.BlockSpec((1, d), const),
            pl.BlockSpec(w_in.shape, const, **single),
            pl.BlockSpec(conv_a_w.shape, const),
            pl.BlockSpec(conv_b_w.shape, const),
            pl.BlockSpec((1, d_b), const),
            pl.BlockSpec((1, d_b), const),
            pl.BlockSpec((1, d_b), const),
            pl.BlockSpec(w_out.shape, const, **single),
        ],
        out_specs=(
            pl.BlockSpec((nseg, seg, d), lambda g, t: (g, t, 0)),
            pl.BlockSpec((nseg, HALO_A, d_a), lambda g, t: (g, 0, 0)),
            pl.BlockSpec((nseg, HALO_B, d_b), lambda g, t: (g, 0, 0)),
        ),
        scratch_shapes=[
            pltpu.VMEM((tm, d), BF16),
            pltpu.VMEM((nseg * (HALO_A + seg), d_a), F32),
            pltpu.VMEM((nseg * (HALO_B + seg), d_b), F32),
            pltpu.VMEM((tm, max(d_a, d_b)), F32),
            pltpu.VMEM((tm, d_a + d_b), BF16),
        ],
        compiler_params=pltpu.CompilerParams(
            dimension_semantics=("arbitrary", "arbitrary"),
            vmem_limit_bytes=VMEM_LIMIT_BYTES,
        ),
        name="mixer",
    )(x3d, state_a, state_b, norm_g, w_in, conv_a_w, conv_b_w, conv_b_bias, ln_g, ln_b, w_out)


def _pad_state(state, halo):
    return jnp.pad(state, ((0, 0), (halo - state.shape[1], 0), (0, 0)))


def _stream(x3d, st_a, st_b, layers, final_norm_g, *, nseg, seg, ffn_tm):
    b, s_len, d = x3d.shape
    new_a, new_b = [], []
    n_layers = len(layers)
    for l, w in enumerate(layers):
        x2d = _ffn(x3d.reshape(b * s_len, d), w["ffn1_norm"], w["ffn1_w_gate"], w["ffn1_w_up"],
                   w["ffn1_w_down"], final_norm_g, final_norm=False, tm=ffn_tm, tf=512)
        x3d, na, nb = _mixer(x2d.reshape(b, s_len, d), _pad_state(st_a[l], HALO_A),
                             _pad_state(st_b[l], HALO_B), w["mix_norm"], w["w_in"], w["conv_a_w"],
                             w["conv_b_w"], w["conv_b_bias"], w["conv_b_ln_g"], w["conv_b_ln_b"],
                             w["w_out"], nseg=nseg, seg=seg)
        x2d = _ffn(x3d.reshape(b * s_len, d), w["ffn2_norm"], w["ffn2_w_gate"], w["ffn2_w_up"],
                   w["ffn2_w_down"], final_norm_g, final_norm=(l == n_layers - 1), tm=ffn_tm,
                   tf=512)
        x3d = x2d.reshape(b, s_len, d)
        new_a.append(na[:, HALO_A - (CONV_A_W - 1):, :])
        new_b.append(nb[:, HALO_B - (CONV_B_W - 1):, :])
    return x3d, jnp.stack(new_a, axis=0), jnp.stack(new_b, axis=0)


def kernel(x_prompt, x_sample, state_conv_a, state_conv_b, ffn1_norm, ffn1_w_gate, ffn1_w_up, ffn1_w_down, mix_norm, w_in, conv_a_w, conv_b_w, conv_b_bias, conv_b_ln_g, conv_b_ln_b, w_out, ffn2_norm, ffn2_w_gate, ffn2_w_up, ffn2_w_down, final_norm):
    depth = w_in.shape[0]
    d_a = conv_a_w.shape[-1]
    d_b = conv_b_w.shape[-1]
    row = lambda v: v.reshape(1, -1)
    layers = []
    for l in range(depth):
        layers.append(dict(
            ffn1_norm=row(ffn1_norm[l]), ffn1_w_gate=ffn1_w_gate[l].astype(BF16),
            ffn1_w_up=ffn1_w_up[l].astype(BF16), ffn1_w_down=ffn1_w_down[l].astype(BF16),
            mix_norm=row(mix_norm[l]), w_in=w_in[l].astype(BF16), conv_a_w=conv_a_w[l],
            conv_b_w=conv_b_w[l], conv_b_bias=row(conv_b_bias[l]), conv_b_ln_g=row(conv_b_ln_g[l]),
            conv_b_ln_b=row(conv_b_ln_b[l]), w_out=w_out[l].astype(BF16),
            ffn2_norm=row(ffn2_norm[l]), ffn2_w_gate=ffn2_w_gate[l].astype(BF16),
            ffn2_w_up=ffn2_w_up[l].astype(BF16), ffn2_w_down=ffn2_w_down[l].astype(BF16)))
    fg = row(final_norm)

    bp = x_prompt.shape[0]
    zero_a = jnp.zeros((depth, bp, CONV_A_W - 1, d_a), x_prompt.dtype)
    zero_b = jnp.zeros((depth, bp, CONV_B_W - 1, d_b), x_prompt.dtype)
    y_p, pa, pb = _stream(x_prompt, zero_a, zero_b, layers, fg, nseg=1, seg=512, ffn_tm=512)
    y_s, sa, sb = _stream(x_sample, state_conv_a, state_conv_b, layers, fg,
                          nseg=4, seg=x_sample.shape[1], ffn_tm=512)
    return (y_p, y_s, pa, pb, sa, sb)
```

```python
import functools

import jax
import jax.numpy as jnp
from jax import lax
from jax.experimental import pallas as pl
from jax.experimental.pallas import tpu as pltpu

EPS = 1e-6
CONV_A_W = 3
CONV_B_W = 31

SUBLANES = 8
LANES = 128
HALO_A = SUBLANES
HALO_B = 4 * SUBLANES
CONV_ROWS = 64
NORM_ROWS = 128
VMEM_LIMIT_BYTES = 60 * 1024 * 1024

F32 = jnp.float32
BF16 = jnp.bfloat16


def _rmsnorm(x, g):
    ms = jnp.mean(x * x, axis=-1, keepdims=True)
    return x * lax.rsqrt(ms + EPS) * g


def _ffn_kernel(x_hbm, g_ref, wg_ref, wu_ref, wd_ref, fg_ref, o_ref, x_buf, x_sem, xn_ref, *,
                final_norm):
    i = pl.program_id(0)
    f = pl.program_id(1)
    tm = x_buf.shape[0]

    def x_copy(tile):
        return pltpu.make_async_copy(x_hbm.at[pl.ds(tile * tm, tm)], x_buf, x_sem)

    @pl.when(jnp.logical_and(i == 0, f == 0))
    def _():
        x_copy(0).start()

    @pl.when(f == 0)
    def _():
        x_copy(i).wait()
        for r in range(0, tm, NORM_ROWS):
            x = x_buf[r:r + NORM_ROWS, :]
            xn_ref[r:r + NORM_ROWS, :] = _rmsnorm(x, g_ref[...]).astype(BF16)
            o_ref[r:r + NORM_ROWS, :] = x

    @pl.when(jnp.logical_and(f == 1, i + 1 < pl.num_programs(0)))
    def _():
        x_copy(i + 1).start()

    xn = xn_ref[...]
    gate = jnp.dot(xn, wg_ref[...], preferred_element_type=F32)
    up = jnp.dot(xn, wu_ref[...], preferred_element_type=F32)
    h = (gate * jax.nn.sigmoid(gate) * up * 0.5).astype(BF16)
    o_ref[...] += jnp.dot(h, wd_ref[...], preferred_element_type=F32)

    if final_norm:
        @pl.when(f == pl.num_programs(1) - 1)
        def _():
            o_ref[...] = _rmsnorm(o_ref[...], fg_ref[...])


def _ffn(x2d, norm_g, wg, wu, wd, final_g, *, final_norm, tm, tf):
    m, d = x2d.shape
    dff = wg.shape[1]
    assert m % tm == 0 and dff % tf == 0 and dff // tf >= 2
    return pl.pallas_call(
        functools.partial(_ffn_kernel, final_norm=final_norm),
        out_shape=jax.ShapeDtypeStruct((m, d), F32),
        grid=(m // tm, dff // tf),
        in_specs=[
            pl.BlockSpec(memory_space=pl.ANY),
            pl.BlockSpec((1, d), lambda i, f: (0, 0)),
            pl.BlockSpec((d, tf), lambda i, f: (0, f)),
            pl.BlockSpec((d, tf), lambda i, f: (0, f)),
            pl.BlockSpec((tf, d), lambda i, f: (f, 0)),
            pl.BlockSpec((1, d), lambda i, f: (0, 0)),
        ],
        out_specs=pl.BlockSpec((tm, d), lambda i, f: (i, 0)),
        scratch_shapes=[
            pltpu.VMEM((tm, d), F32),
            pltpu.SemaphoreType.DMA,
            pltpu.VMEM((tm, d), BF16),
        ],
        compiler_params=pltpu.CompilerParams(
            dimension_semantics=("arbitrary", "arbitrary"),
            vmem_limit_bytes=VMEM_LIMIT_BYTES,
        ),
        name="ffn_final" if final_norm else "ffn",
    )(x2d, norm_g, wg, wu, wd, final_g)


def _shifted_sum(parts, rows):
    out = parts[0][SUBLANES:SUBLANES + rows]
    for b in range(1, len(parts)):
        out = out + parts[b][SUBLANES - b:SUBLANES - b + rows]
    return out


def _mixer_kernel(x_ref, sa_ref, sb_ref, g_ref, win_ref, caw_ref, cbw_ref, cbb_ref, lng_ref,
                  lnb_ref, wout_ref, o_ref, na_ref, nb_ref,
                  hn_ref, exta_ref, extb_ref, y_ref, *, nseg, seg, cw):
    t = pl.program_id(1)
    tm = nseg * seg
    d_model = x_ref.shape[-1]
    d_a = exta_ref.shape[-1]
    d_b = extb_ref.shape[-1]
    stride_a = HALO_A + seg
    stride_b = HALO_B + seg
    items = [(s, r) for s in range(nseg) for r in range(seg - CONV_ROWS, -1, -CONV_ROWS)]

    @pl.when(t == 0)
    def _():
        for s in range(nseg):
            exta_ref[s * stride_a:s * stride_a + HALO_A, :] = sa_ref[s]
            extb_ref[s * stride_b:s * stride_b + HALO_B, :] = sb_ref[s]

    @pl.when(t > 0)
    def _():
        for s in range(nseg):
            exta_ref[s * stride_a:s * stride_a + HALO_A, :] = na_ref[s]
            extb_ref[s * stride_b:s * stride_b + HALO_B, :] = nb_ref[s]

    hn_ref[...] = _rmsnorm(x_ref[...].reshape(tm, d_model), g_ref[...]).astype(BF16)

    def proj(col0):
        return jnp.dot(hn_ref[...], win_ref[:, col0:col0 + cw], preferred_element_type=F32)

    def put(ext_ref, new_ref, stride, halo, cols, val):
        for s in range(nseg):
            ext_ref[s * stride + halo:(s + 1) * stride, cols] = val[s * seg:(s + 1) * seg]
            new_ref[s, :, cols] = val[(s + 1) * seg - halo:(s + 1) * seg]

    for c in range(d_b // cw):
        u = proj(3 * d_a + c * cw)
        u = u * jax.nn.sigmoid(proj(3 * d_a + d_b + c * cw))
        put(extb_ref, nb_ref, stride_b, HALO_B, slice(c * cw, (c + 1) * cw), u)

    for c in range(d_a // cw):
        cv = proj(d_a + c * cw) * proj(2 * d_a + c * cw)
        put(exta_ref, na_ref, stride_a, HALO_A, slice(c * cw, (c + 1) * cw), cv)

    n_groups = HALO_B // SUBLANES
    for s, r in items:
        e0 = s * stride_b + r
        for l in range(d_b // LANES):
            ls = slice(l * LANES, (l + 1) * LANES)
            win = extb_ref[e0:e0 + CONV_ROWS + HALO_B, ls]
            parts = []
            for b in range(SUBLANES):
                acc = None
                for a in range(n_groups):
                    j = SUBLANES * a + b
                    if j >= CONV_B_W:
                        continue
                    lo = HALO_B - SUBLANES - SUBLANES * a
                    term = (win[lo:lo + CONV_ROWS + SUBLANES]
                            * cbw_ref[CONV_B_W - 1 - j:CONV_B_W - j, ls])
                    acc = term if acc is None else acc + term
                parts.append(acc)
            extb_ref[e0 + HALO_B:e0 + HALO_B + CONV_ROWS, ls] = (
                _shifted_sum(parts, CONV_ROWS) + cbb_ref[:, ls])
        v = extb_ref[e0 + HALO_B:e0 + HALO_B + CONV_ROWS, :]
        mu = jnp.mean(v, axis=-1, keepdims=True)
        vc = v - mu
        var = jnp.mean(vc * vc, axis=-1, keepdims=True)
        n = vc * lax.rsqrt(var + EPS) * lng_ref[...] + lnb_ref[...]
        y_ref[s * seg + r:s * seg + r + CONV_ROWS, d_a:d_a + d_b] = (
            n * jax.nn.sigmoid(n)).astype(BF16)

    for s, r in items:
        e0 = s * stride_a + r
        for l in range(d_a // LANES):
            ls = slice(l * LANES, (l + 1) * LANES)
            win = exta_ref[e0:e0 + CONV_ROWS + HALO_A, ls]
            out = win[HALO_A:HALO_A + CONV_ROWS] * caw_ref[CONV_A_W - 1:CONV_A_W, ls]
            for j in range(1, CONV_A_W):
                out = out + (win[HALO_A - j:HALO_A - j + CONV_ROWS]
                             * caw_ref[CONV_A_W - 1 - j:CONV_A_W - j, ls])
            exta_ref[e0 + HALO_A:e0 + HALO_A + CONV_ROWS, ls] = out

    for c in range(d_a // cw):
        cs = slice(c * cw, (c + 1) * cw)
        b_gate = proj(c * cw)
        for s in range(nseg):
            y_ref[s * seg:(s + 1) * seg, cs] = (
                b_gate[s * seg:(s + 1) * seg]
                * exta_ref[s * stride_a + HALO_A:(s + 1) * stride_a, cs]).astype(BF16)

    out = jnp.dot(y_ref[...], wout_ref[...], preferred_element_type=F32)
    o_ref[...] = x_ref[...] + out.reshape(o_ref.shape)


def _mixer(x3d, state_a, state_b, norm_g, w_in, conv_a_w, conv_b_w, conv_b_bias, ln_g, ln_b,
           w_out, *, nseg, seg, cw=256):
    b, s_len, d = x3d.shape
    d_a = conv_a_w.shape[-1]
    d_b = conv_b_w.shape[-1]
    assert b % nseg == 0 and s_len % seg == 0 and seg % CONV_ROWS == 0
    assert seg >= HALO_B and (nseg == 1 or s_len == seg)
    tm = nseg * seg
    const = lambda g, t: (0, 0)
    single = dict(pipeline_mode=pl.Buffered(1))
    return pl.pallas_call(
        functools.partial(_mixer_kernel, nseg=nseg, seg=seg, cw=cw),
        out_shape=(
            jax.ShapeDtypeStruct((b, s_len, d), F32),
            jax.ShapeDtypeStruct((b, HALO_A, d_a), F32),
            jax.ShapeDtypeStruct((b, HALO_B, d_b), F32),
        ),
        grid=(b // nseg, s_len // seg),
        in_specs=[
            pl.BlockSpec((nseg, seg, d), lambda g, t: (g, t, 0)),
            pl.BlockSpec((nseg, HALO_A, d_a), lambda g, t: (g, 0, 0)),
            pl.BlockSpec((nseg, HALO_B, d_b), lambda g, t: (g, 0, 0)),
            pl.BlockSpec((1, d), const),
            pl.BlockSpec(w_in.shape, const, **single),
            pl.BlockSpec(conv_a_w.shape, const),
            pl.BlockSpec(conv_b_w.shape, const),
            pl.BlockSpec((1, d_b), const),
            pl.BlockSpec((1, d_b), const),
            pl.BlockSpec((1, d_b), const),
            pl.BlockSpec(w_out.shape, const, **single),
        ],
        out_specs=(
            pl.BlockSpec((nseg, seg, d), lambda g, t: (g, t, 0)),
            pl.BlockSpec((nseg, HALO_A, d_a), lambda g, t: (g, 0, 0)),
            pl.BlockSpec((nseg, HALO_B, d_b), lambda g, t: (g, 0, 0)),
        ),
        scratch_shapes=[
            pltpu.VMEM((tm, d), BF16),
            pltpu.VMEM((nseg * (HALO_A + seg), d_a), F32),
            pltpu.VMEM((nseg * (HALO_B + seg), d_b), F32),
            pltpu.VMEM((tm, d_a + d_b), BF16),
        ],
        compiler_params=pltpu.CompilerParams(
            dimension_semantics=("arbitrary", "arbitrary"),
            vmem_limit_bytes=VMEM_LIMIT_BYTES,
        ),
        name="mixer",
    )(x3d, state_a, state_b, norm_g, w_in, conv_a_w, conv_b_w, conv_b_bias, ln_g, ln_b, w_out)


def _pad_state(state, halo):
    return jnp.pad(state, ((0, 0), (halo - state.shape[1], 0), (0, 0)))


def _stream(x3d, st_a, st_b, layers, final_norm_g, *, nseg, seg, ffn_tm):
    b, s_len, d = x3d.shape
    new_a, new_b = [], []
    n_layers = len(layers)
    for l, w in enumerate(layers):
        x2d = _ffn(x3d.reshape(b * s_len, d), w["ffn1_norm"], w["ffn1_w_gate"], w["ffn1_w_up"],
                   w["ffn1_w_down"], final_norm_g, final_norm=False, tm=ffn_tm, tf=512)
        x3d, na, nb = _mixer(x2d.reshape(b, s_len, d), _pad_state(st_a[l], HALO_A),
                             _pad_state(st_b[l], HALO_B), w["mix_norm"], w["w_in"], w["conv_a_w"],
                             w["conv_b_w"], w["conv_b_bias"], w["conv_b_ln_g"], w["conv_b_ln_b"],
                             w["w_out"], nseg=nseg, seg=seg)
        x2d = _ffn(x3d.reshape(b * s_len, d), w["ffn2_norm"], w["ffn2_w_gate"], w["ffn2_w_up"],
                   w["ffn2_w_down"], final_norm_g, final_norm=(l == n_layers - 1), tm=ffn_tm,
                   tf=512)
        x3d = x2d.reshape(b, s_len, d)
        new_a.append(na[:, HALO_A - (CONV_A_W - 1):, :])
        new_b.append(nb[:, HALO_B - (CONV_B_W - 1):, :])
    return x3d, jnp.stack(new_a, axis=0), jnp.stack(new_b, axis=0)


def kernel(x_prompt, x_sample, state_conv_a, state_conv_b, ffn1_norm, ffn1_w_gate, ffn1_w_up, ffn1_w_down, mix_norm, w_in, conv_a_w, conv_b_w, conv_b_bias, conv_b_ln_g, conv_b_ln_b, w_out, ffn2_norm, ffn2_w_gate, ffn2_w_up, ffn2_w_down, final_norm):
    depth = w_in.shape[0]
    d_a = conv_a_w.shape[-1]
    d_b = conv_b_w.shape[-1]
    row = lambda v: v.reshape(1, -1)
    layers = []
    for l in range(depth):
        layers.append(dict(
            ffn1_norm=row(ffn1_norm[l]), ffn1_w_gate=ffn1_w_gate[l].astype(BF16),
            ffn1_w_up=ffn1_w_up[l].astype(BF16), ffn1_w_down=ffn1_w_down[l].astype(BF16),
            mix_norm=row(mix_norm[l]), w_in=w_in[l].astype(BF16), conv_a_w=conv_a_w[l],
            conv_b_w=conv_b_w[l], conv_b_bias=row(conv_b_bias[l]), conv_b_ln_g=row(conv_b_ln_g[l]),
            conv_b_ln_b=row(conv_b_ln_b[l]), w_out=w_out[l].astype(BF16),
            ffn2_norm=row(ffn2_norm[l]), ffn2_w_gate=ffn2_w_gate[l].astype(BF16),
            ffn2_w_up=ffn2_w_up[l].astype(BF16), ffn2_w_down=ffn2_w_down[l].astype(BF16)))
    fg = row(final_norm)

    bp = x_prompt.shape[0]
    zero_a = jnp.zeros((depth, bp, CONV_A_W - 1, d_a), x_prompt.dtype)
    zero_b = jnp.zeros((depth, bp, CONV_B_W - 1, d_b), x_prompt.dtype)
    y_p, pa, pb = _stream(x_prompt, zero_a, zero_b, layers, fg, nseg=1, seg=512, ffn_tm=1024)
    y_s, sa, sb = _stream(x_sample, state_conv_a, state_conv_b, layers, fg,
                          nseg=4, seg=x_sample.shape[1], ffn_tm=1024)
    return (y_p, y_s, pa, pb, sa, sb)
```

```python
import functools

import jax
import jax.numpy as jnp
from jax import lax
from jax.experimental import pallas as pl
from jax.experimental.pallas import tpu as pltpu

EPS = 1e-6
CONV_A_W = 3
CONV_B_W = 31

SUBLANES = 8
LANES = 128
HALO_A = SUBLANES
HALO_B = 4 * SUBLANES
CONV_ROWS = 64
NORM_ROWS = 128
CAST_COLS = 4 * LANES
VMEM_LIMIT_BYTES = 60 * 1024 * 1024

F32 = jnp.float32
BF16 = jnp.bfloat16


def _rmsnorm(x, g):
    ms = jnp.mean(x * x, axis=-1, keepdims=True)
    return x * lax.rsqrt(ms + EPS) * g


def _ffn_kernel(x_hbm, g_ref, wg_ref, wu_ref, wd_ref, fg_ref, *rest, final_norm, n_cast):
    cast_in, o_ref, cast_out = rest[:n_cast], rest[n_cast], rest[n_cast + 1:2 * n_cast + 1]
    x_buf, x_sem, xn_ref = rest[2 * n_cast + 1:]
    for w_ref, wb_ref in zip(cast_in, cast_out):
        wb_ref[...] = w_ref[...].astype(BF16)

    i = pl.program_id(0)
    f = pl.program_id(1)
    tm = x_buf.shape[0]

    def x_copy(tile):
        return pltpu.make_async_copy(x_hbm.at[pl.ds(tile * tm, tm)], x_buf, x_sem)

    @pl.when(jnp.logical_and(i == 0, f == 0))
    def _():
        x_copy(0).start()

    @pl.when(f == 0)
    def _():
        x_copy(i).wait()
        for r in range(0, tm, NORM_ROWS):
            x = x_buf[r:r + NORM_ROWS, :]
            xn_ref[r:r + NORM_ROWS, :] = _rmsnorm(x, g_ref[...]).astype(BF16)
            o_ref[r:r + NORM_ROWS, :] = x

    @pl.when(jnp.logical_and(f == 1, i + 1 < pl.num_programs(0)))
    def _():
        x_copy(i + 1).start()

    xn = xn_ref[...]
    gate = jnp.dot(xn, wg_ref[...], preferred_element_type=F32)
    up = jnp.dot(xn, wu_ref[...], preferred_element_type=F32)
    h = (gate * jax.nn.sigmoid(gate) * up * 0.5).astype(BF16)
    o_ref[...] += jnp.dot(h, wd_ref[...], preferred_element_type=F32)

    if final_norm:
        @pl.when(f == pl.num_programs(1) - 1)
        def _():
            o_ref[...] = _rmsnorm(o_ref[...], fg_ref[...])


def _ffn(x2d, norm_g, wg, wu, wd, final_g, *, final_norm, tm, tf, cast=()):
    m, d = x2d.shape
    dff = wg.shape[1]
    ni, nf = m // tm, dff // tf
    assert m % tm == 0 and dff % tf == 0 and nf >= 2
    cast_specs = []
    for w in cast:
        rows, cols = w.shape
        assert rows % (ni * 2 * SUBLANES) == 0 and cols % CAST_COLS == 0 and cols // CAST_COLS <= nf
        last = cols // CAST_COLS - 1
        cast_specs.append(pl.BlockSpec((rows // ni, CAST_COLS),
                                       lambda i, f, last=last: (i, jnp.minimum(f, last))))
    return pl.pallas_call(
        functools.partial(_ffn_kernel, final_norm=final_norm, n_cast=len(cast)),
        out_shape=(jax.ShapeDtypeStruct((m, d), F32),
                   *[jax.ShapeDtypeStruct(w.shape, BF16) for w in cast]),
        grid=(ni, nf),
        in_specs=[
            pl.BlockSpec(memory_space=pl.ANY),
            pl.BlockSpec((1, d), lambda i, f: (0, 0)),
            pl.BlockSpec((d, tf), lambda i, f: (0, f)),
            pl.BlockSpec((d, tf), lambda i, f: (0, f)),
            pl.BlockSpec((tf, d), lambda i, f: (f, 0)),
            pl.BlockSpec((1, d), lambda i, f: (0, 0)),
            *cast_specs,
        ],
        out_specs=(pl.BlockSpec((tm, d), lambda i, f: (i, 0)), *cast_specs),
        scratch_shapes=[
            pltpu.VMEM((tm, d), F32),
            pltpu.SemaphoreType.DMA,
            pltpu.VMEM((tm, d), BF16),
        ],
        compiler_params=pltpu.CompilerParams(
            dimension_semantics=("arbitrary", "arbitrary"),
            vmem_limit_bytes=VMEM_LIMIT_BYTES,
        ),
        name="ffn_final" if final_norm else "ffn",
    )(x2d, norm_g, wg, wu, wd, final_g, *cast)


def _shifted_sum(parts, rows):
    out = parts[0][SUBLANES:SUBLANES + rows]
    for b in range(1, len(parts)):
        out = out + parts[b][SUBLANES - b:SUBLANES - b + rows]
    return out


def _mixer_kernel(x_ref, sa_ref, sb_ref, g_ref, win_ref, caw_ref, cbw_ref, cbb_ref, lng_ref,
                  lnb_ref, wout_ref, o_ref, na_ref, nb_ref,
                  hn_ref, exta_ref, extb_ref, y_ref, *, nseg, seg, cw):
    t = pl.program_id(1)
    tm = nseg * seg
    d_model = x_ref.shape[-1]
    d_a = exta_ref.shape[-1]
    d_b = extb_ref.shape[-1]
    stride_a = HALO_A + seg
    stride_b = HALO_B + seg
    items = [(s, r) for s in range(nseg) for r in range(seg - CONV_ROWS, -1, -CONV_ROWS)]

    @pl.when(t == 0)
    def _():
        for s in range(nseg):
            exta_ref[s * stride_a:s * stride_a + HALO_A, :] = sa_ref[s]
            extb_ref[s * stride_b:s * stride_b + HALO_B, :] = sb_ref[s]

    @pl.when(t > 0)
    def _():
        for s in range(nseg):
            exta_ref[s * stride_a:s * stride_a + HALO_A, :] = na_ref[s]
            extb_ref[s * stride_b:s * stride_b + HALO_B, :] = nb_ref[s]

    hn_ref[...] = _rmsnorm(x_ref[...].reshape(tm, d_model), g_ref[...]).astype(BF16)

    def proj(col0):
        return jnp.dot(hn_ref[...], win_ref[:, col0:col0 + cw], preferred_element_type=F32)

    def put(ext_ref, new_ref, stride, halo, cols, val):
        for s in range(nseg):
            ext_ref[s * stride + halo:(s + 1) * stride, cols] = val[s * seg:(s + 1) * seg]
            new_ref[s, :, cols] = val[(s + 1) * seg - halo:(s + 1) * seg]

    for c in range(d_b // cw):
        u = proj(3 * d_a + c * cw)
        u = u * jax.nn.sigmoid(proj(3 * d_a + d_b + c * cw))
        put(extb_ref, nb_ref, stride_b, HALO_B, slice(c * cw, (c + 1) * cw), u)

    for c in range(d_a // cw):
        cv = proj(d_a + c * cw) * proj(2 * d_a + c * cw)
        put(exta_ref, na_ref, stride_a, HALO_A, slice(c * cw, (c + 1) * cw), cv)

    n_groups = HALO_B // SUBLANES
    for s, r in items:
        e0 = s * stride_b + r
        for l in range(d_b // LANES):
            ls = slice(l * LANES, (l + 1) * LANES)
            win = extb_ref[e0:e0 + CONV_ROWS + HALO_B, ls]
            parts = []
            for b in range(SUBLANES):
                acc = None
                for a in range(n_groups):
                    j = SUBLANES * a + b
                    if j >= CONV_B_W:
                        continue
                    lo = HALO_B - SUBLANES - SUBLANES * a
                    term = (win[lo:lo + CONV_ROWS + SUBLANES]
                            * cbw_ref[CONV_B_W - 1 - j:CONV_B_W - j, ls])
                    acc = term if acc is None else acc + term
                parts.append(acc)
            extb_ref[e0 + HALO_B:e0 + HALO_B + CONV_ROWS, ls] = (
                _shifted_sum(parts, CONV_ROWS) + cbb_ref[:, ls])
        v = extb_ref[e0 + HALO_B:e0 + HALO_B + CONV_ROWS, :]
        mu = jnp.mean(v, axis=-1, keepdims=True)
        vc = v - mu
        var = jnp.mean(vc * vc, axis=-1, keepdims=True)
        n = vc * lax.rsqrt(var + EPS) * lng_ref[...] + lnb_ref[...]
        y_ref[s * seg + r:s * seg + r + CONV_ROWS, d_a:d_a + d_b] = (
            n * jax.nn.sigmoid(n)).astype(BF16)

    for s, r in items:
        e0 = s * stride_a + r
        for l in range(d_a // LANES):
            ls = slice(l * LANES, (l + 1) * LANES)
            win = exta_ref[e0:e0 + CONV_ROWS + HALO_A, ls]
            out = win[HALO_A:HALO_A + CONV_ROWS] * caw_ref[CONV_A_W - 1:CONV_A_W, ls]
            for j in range(1, CONV_A_W):
                out = out + (win[HALO_A - j:HALO_A - j + CONV_ROWS]
                             * caw_ref[CONV_A_W - 1 - j:CONV_A_W - j, ls])
            exta_ref[e0 + HALO_A:e0 + HALO_A + CONV_ROWS, ls] = out

    for c in range(d_a // cw):
        cs = slice(c * cw, (c + 1) * cw)
        b_gate = proj(c * cw)
        for s in range(nseg):
            y_ref[s * seg:(s + 1) * seg, cs] = (
                b_gate[s * seg:(s + 1) * seg]
                * exta_ref[s * stride_a + HALO_A:(s + 1) * stride_a, cs]).astype(BF16)

    out = jnp.dot(y_ref[...], wout_ref[...], preferred_element_type=F32)
    o_ref[...] = x_ref[...] + out.reshape(o_ref.shape)


def _mixer(x3d, state_a, state_b, norm_g, w_in, conv_a_w, conv_b_w, conv_b_bias, ln_g, ln_b,
           w_out, *, nseg, seg, cw=256):
    b, s_len, d = x3d.shape
    d_a = conv_a_w.shape[-1]
    d_b = conv_b_w.shape[-1]
    assert b % nseg == 0 and s_len % seg == 0 and seg % CONV_ROWS == 0
    assert seg >= HALO_B and (nseg == 1 or s_len == seg)
    tm = nseg * seg
    const = lambda g, t: (0, 0)
    single = dict(pipeline_mode=pl.Buffered(1))
    return pl.pallas_call(
        functools.partial(_mixer_kernel, nseg=nseg, seg=seg, cw=cw),
        out_shape=(
            jax.ShapeDtypeStruct((b, s_len, d), F32),
            jax.ShapeDtypeStruct((b, HALO_A, d_a), F32),
            jax.ShapeDtypeStruct((b, HALO_B, d_b), F32),
        ),
        grid=(b // nseg, s_len // seg),
        in_specs=[
            pl.BlockSpec((nseg, seg, d), lambda g, t: (g, t, 0)),
            pl.BlockSpec((nseg, HALO_A, d_a), lambda g, t: (g, 0, 0)),
            pl.BlockSpec((nseg, HALO_B, d_b), lambda g, t: (g, 0, 0)),
            pl.BlockSpec((1, d), const),
            pl.BlockSpec(w_in.shape, const, **single),
            pl.BlockSpec(conv_a_w.shape, const),
            pl.BlockSpec(conv_b_w.shape, const),
            pl.BlockSpec((1, d_b), const),
            pl.BlockSpec((1, d_b), const),
            pl.BlockSpec((1, d_b), const),
            pl.BlockSpec(w_out.shape, const, **single),
        ],
        out_specs=(
            pl.BlockSpec((nseg, seg, d), lambda g, t: (g, t, 0)),
            pl.BlockSpec((nseg, HALO_A, d_a), lambda g, t: (g, 0, 0)),
            pl.BlockSpec((nseg, HALO_B, d_b), lambda g, t: (g, 0, 0)),
        ),
        scratch_shapes=[
            pltpu.VMEM((tm, d), BF16),
            pltpu.VMEM((nseg * (HALO_A + seg), d_a), F32),
            pltpu.VMEM((nseg * (HALO_B + seg), d_b), F32),
            pltpu.VMEM((tm, d_a + d_b), BF16),
        ],
        compiler_params=pltpu.CompilerParams(
            dimension_semantics=("arbitrary", "arbitrary"),
            vmem_limit_bytes=VMEM_LIMIT_BYTES,
        ),
        name="mixer",
    )(x3d, state_a, state_b, norm_g, w_in, conv_a_w, conv_b_w, conv_b_bias, ln_g, ln_b, w_out)


def _pad_state(state, halo):
    return jnp.pad(state, ((0, 0), (halo - state.shape[1], 0), (0, 0)))


LATE_WEIGHTS = ("w_in", "w_out", "ffn2_w_gate", "ffn2_w_up", "ffn2_w_down")


def _stream(x3d, st_a, st_b, layers, final_norm_g, *, nseg, seg, ffn_tm, cast_late):
    b, s_len, d = x3d.shape
    new_a, new_b, done = [], [], []
    n_layers = len(layers)
    for l, w in enumerate(layers):
        late = tuple(w[k] for k in LATE_WEIGHTS) if cast_late else ()
        x2d, *late_bf16 = _ffn(x3d.reshape(b * s_len, d), w["ffn1_norm"], w["ffn1_w_gate"],
                               w["ffn1_w_up"], w["ffn1_w_down"], final_norm_g, final_norm=False,
                               tm=ffn_tm, tf=512, cast=late)
        if cast_late:
            w = dict(w, **dict(zip(LATE_WEIGHTS, late_bf16)))
        done.append(w)
        x3d, na, nb = _mixer(x2d.reshape(b, s_len, d), _pad_state(st_a[l], HALO_A),
                             _pad_state(st_b[l], HALO_B), w["mix_norm"], w["w_in"], w["conv_a_w"],
                             w["conv_b_w"], w["conv_b_bias"], w["conv_b_ln_g"], w["conv_b_ln_b"],
                             w["w_out"], nseg=nseg, seg=seg)
        (x2d,) = _ffn(x3d.reshape(b * s_len, d), w["ffn2_norm"], w["ffn2_w_gate"], w["ffn2_w_up"],
                      w["ffn2_w_down"], final_norm_g, final_norm=(l == n_layers - 1), tm=ffn_tm,
                      tf=512)
        x3d = x2d.reshape(b, s_len, d)
        new_a.append(na[:, HALO_A - (CONV_A_W - 1):, :])
        new_b.append(nb[:, HALO_B - (CONV_B_W - 1):, :])
    return x3d, jnp.stack(new_a, axis=0), jnp.stack(new_b, axis=0), done


def kernel(x_prompt, x_sample, state_conv_a, state_conv_b, ffn1_norm, ffn1_w_gate, ffn1_w_up, ffn1_w_down, mix_norm, w_in, conv_a_w, conv_b_w, conv_b_bias, conv_b_ln_g, conv_b_ln_b, w_out, ffn2_norm, ffn2_w_gate, ffn2_w_up, ffn2_w_down, final_norm):
    depth = w_in.shape[0]
    d_a = conv_a_w.shape[-1]
    d_b = conv_b_w.shape[-1]
    row = lambda v: v.reshape(1, -1)
    layers = []
    for l in range(depth):
        layers.append(dict(
            ffn1_norm=row(ffn1_norm[l]), ffn1_w_gate=ffn1_w_gate[l].astype(BF16),
            ffn1_w_up=ffn1_w_up[l].astype(BF16), ffn1_w_down=ffn1_w_down[l].astype(BF16),
            mix_norm=row(mix_norm[l]), w_in=w_in[l], conv_a_w=conv_a_w[l],
            conv_b_w=conv_b_w[l], conv_b_bias=row(conv_b_bias[l]), conv_b_ln_g=row(conv_b_ln_g[l]),
            conv_b_ln_b=row(conv_b_ln_b[l]), w_out=w_out[l],
            ffn2_norm=row(ffn2_norm[l]), ffn2_w_gate=ffn2_w_gate[l],
            ffn2_w_up=ffn2_w_up[l], ffn2_w_down=ffn2_w_down[l]))
    fg = row(final_norm)

    bp = x_prompt.shape[0]
    zero_a = jnp.zeros((depth, bp, CONV_A_W - 1, d_a), x_prompt.dtype)
    zero_b = jnp.zeros((depth, bp, CONV_B_W - 1, d_b), x_prompt.dtype)
    y_p, pa, pb, layers = _stream(x_prompt, zero_a, zero_b, layers, fg, nseg=1, seg=512,
                                  ffn_tm=1024, cast_late=True)
    y_s, sa, sb, _ = _stream(x_sample, state_conv_a, state_conv_b, layers, fg,
                             nseg=4, seg=x_sample.shape[1], ffn_tm=1024, cast_late=False)
    return (y_p, y_s, pa, pb, sa, sb)
```
